```python
import math
import jax, jax.numpy as jnp
from jax import lax
import numpy as np

D_MODEL = 4096
BATCH = 1
SEQ = 8192
DEPTH = 2

N_A_LAYERS = DEPTH // 2
N_B_LAYERS = DEPTH - N_A_LAYERS
EPS = 1e-6
NEG = -1e30
FORCE = 1e30
A_WIDTH = D_MODEL
A_CHUNK = 128
A_GROUPS = 32
A_GROUP_DIM = A_WIDTH // A_GROUPS
HEAD_DIM = 128
N_HEADS = D_MODEL // HEAD_DIM
N_KV_GROUPS = 4
HEADS_PER_GROUP = N_HEADS // N_KV_GROUPS
CMP_LEN = 32
CMP_STRIDE = 16
SLC_BLOCK = 64
SLC_TOPN = 16
WINDOW = 512
Q_BLOCK = 128
N_BRANCH = 3
N_BUCKETS = 32
MAX_DISTANCE = 1024
D_FF = 11008
CONV_WIDTH = 3

kernel_name = "yoco_gmlp_nsa_convffn_hybrid"


def rmsnorm(x, g):
    xf = x.astype(jnp.float32)
    y = xf * lax.rsqrt(jnp.mean(xf * xf, axis=-1, keepdims=True) + EPS)
    return (y * g.astype(jnp.float32)).astype(x.dtype)


def layernorm(x, g, b):
    xf = x.astype(jnp.float32)
    mu = jnp.mean(xf, axis=-1, keepdims=True)
    var = jnp.mean(jnp.square(xf - mu), axis=-1, keepdims=True)
    y = (xf - mu) * lax.rsqrt(var + EPS)
    return (y * g.astype(jnp.float32) + b.astype(jnp.float32)).astype(x.dtype)


def t5_bucket(rel):
    n = jnp.maximum(rel, 0)
    max_exact = N_BUCKETS // 2
    nf = jnp.maximum(n, max_exact).astype(jnp.float32)
    large = max_exact + (jnp.log(nf / max_exact) / math.log(MAX_DISTANCE / max_exact)
                         * (N_BUCKETS - max_exact)).astype(jnp.int32)
    large = jnp.minimum(large, N_BUCKETS - 1)
    return jnp.where(n < max_exact, n, large)


def residual_sublayer(x, c_act, w_mod, b_mod, g_pre, g_post, fn):
    shift, scale, gate = jnp.split(c_act @ w_mod + b_mod, 3, axis=-1)
    h = rmsnorm(x, g_pre) * (1 + scale[:, None, :]) + shift[:, None, :]
    return x + gate[:, None, :] * rmsnorm(fn(h), g_post)


def chunked_gmlp(h, w_in, b_in, ln_g, ln_b, w_s, b_s, w_out):
    B, S, _ = h.shape
    z = jax.nn.gelu(h @ w_in + b_in)
    u, v = jnp.split(z, 2, axis=-1)
    v = layernorm(v, ln_g, ln_b)
    v = v.reshape(B, S // A_CHUNK, A_CHUNK, A_GROUPS, A_GROUP_DIM)
    causal = jnp.tril(jnp.ones((A_CHUNK, A_CHUNK), dtype=bool))
    ws = jnp.where(causal[None], w_s, 0)
    mixed = jnp.einsum('gts,bcsgd->bctgd', ws, v) + b_s.T[None, None, :, :, None]
    return (u * mixed.reshape(B, S, A_WIDTH)) @ w_out


def conv_ffn(h, w_gate, w_up, conv_w, conv_b, w_down):
    S = h.shape[1]
    a = h @ w_gate
    ap = jnp.pad(a, ((0, 0), (CONV_WIDTH - 1, 0), (0, 0)))
    a = sum(ap[:, k:k + S] * conv_w[k] for k in range(CONV_WIDTH)) + conv_b
    return (jax.nn.silu(a) * (h @ w_up)) @ w_down


def compress_blocks(raw, pe, w1, b1, w2):
    B, S, G, DH = raw.shape
    m_len = CMP_LEN // CMP_STRIDE
    n_w = S // CMP_STRIDE
    n_full = n_w - m_len + 1
    r = raw.reshape(B, n_w, CMP_STRIDE, G, DH)
    blocks = jnp.concatenate([r[:, i:i + n_full] for i in range(m_len)], axis=2)
    blocks = blocks + pe[None, None, :, None, :]
    flat = blocks.transpose(0, 1, 3, 2, 4).reshape(B, n_full, G, CMP_LEN * DH)
    out = jax.nn.gelu(flat @ w1 + b1) @ w2
    return jnp.pad(out, ((0, 0), (0, m_len - 1), (0, 0), (0, 0)))


def shared_kv(xs, c_act, kv_norm, kv_mod_w, kv_mod_b, w_kv, cmp_pe, cmp_w1, cmp_b1, cmp_w2):
    B, S, _ = xs.shape
    G, DH = N_KV_GROUPS, HEAD_DIM
    shift, scale = jnp.split(c_act @ kv_mod_w + kv_mod_b, 2, axis=-1)
    h = rmsnorm(xs, kv_norm) * (1 + scale[:, None, :]) + shift[:, None, :]
    kv = (h @ w_kv).reshape(B, S, 2 * N_BRANCH, G, DH)
    kc = compress_blocks(kv[:, :, 0], cmp_pe[0], cmp_w1[0], cmp_b1[0], cmp_w2[0])
    vc = compress_blocks(kv[:, :, 1], cmp_pe[1], cmp_w1[1], cmp_b1[1], cmp_w2[1])
    n_slc = S // SLC_BLOCK
    to_blocks = lambda t: t.reshape(B, n_slc, SLC_BLOCK, G, DH).transpose(0, 3, 1, 2, 4)
    k_slc, v_slc = to_blocks(kv[:, :, 2]), to_blocks(kv[:, :, 3])
    pad_w = lambda t: jnp.pad(t, ((0, 0), (WINDOW, 0), (0, 0), (0, 0)))
    k_win, v_win = pad_w(kv[:, :, 4]), pad_w(kv[:, :, 5])
    return (kc, vc, k_slc, v_slc, k_win, v_win)


_gather_blocks = jax.vmap(jax.vmap(lambda kb, ix: kb[ix]))


def nsa_attention(h, kvs, w_in, b_in, w_out, rel_bias):
    B, S, _ = h.shape
    G, HG, DH = N_KV_GROUPS, HEADS_PER_GROUP, HEAD_DIM
    kc, vc, k_slc, v_slc, k_win, v_win = kvs
    proj = h @ w_in + b_in
    q = proj[..., :N_HEADS * DH].reshape(B, S, G, HG, DH)
    gates = jax.nn.sigmoid(proj[..., N_HEADS * DH:].astype(jnp.float32)).reshape(B, S, G, HG, N_BRANCH)
    nq = S // Q_BLOCK
    q_blocks = jnp.moveaxis(q.reshape(B, nq, Q_BLOCK, G, HG, DH), 1, 0)
    g_blocks = jnp.moveaxis(gates.reshape(B, nq, Q_BLOCK, G, HG, N_BRANCH), 1, 0)

    n_cmp = kc.shape[1]
    n_slc = S // SLC_BLOCK
    n_sel = min(SLC_TOPN, n_slc)
    m_sub = SLC_BLOCK // CMP_STRIDE
    m_len = CMP_LEN // CMP_STRIDE
    cmp_end = jnp.arange(n_cmp) * CMP_STRIDE + CMP_LEN - 1
    tb = rel_bias.reshape(N_BUCKETS, G, HG)
    g_ix = jnp.arange(G)[None, :, None, None]
    scale = HEAD_DIM ** -0.5
    t_loc = jnp.arange(Q_BLOCK)
    k_loc = jnp.arange(Q_BLOCK + WINDOW)
    win_rel = t_loc[:, None] + WINDOW - k_loc[None, :]
    win_bias = tb[t5_bucket(win_rel)].transpose(2, 3, 0, 1)
    blk = jnp.arange(n_slc)

    def block_fn(args):
        qb, gb, bi = args
        q0 = bi * Q_BLOCK
        t = q0 + t_loc
        rel_c = t[:, None] - cmp_end[None, :]
        valid_c = rel_c >= 0
        s_c = jnp.einsum('btghd,bngd->bghtn', qb, kc).astype(jnp.float32) * scale
        s_c = jnp.where(valid_c, s_c + tb[t5_bucket(rel_c)].transpose(2, 3, 0, 1), NEG)
        p_c = jax.nn.softmax(s_c, axis=-1) * valid_c
        o_c = jnp.einsum('bghtn,bngd->btghd', p_c.astype(vc.dtype), vc)
        imp = p_c.sum(axis=2)
        imp_slc = imp.reshape(B, G, Q_BLOCK, n_slc, m_sub).sum(-1)
        imp_pad = jnp.pad(imp, ((0, 0), (0, 0), (0, 0), (m_sub, 0))).reshape(B, G, Q_BLOCK, n_slc + 1, m_sub)
        for r in range(1, m_len):
            imp_slc = imp_slc + imp_pad[..., :n_slc, m_sub - r]
        cur = t // SLC_BLOCK
        valid_s = blk[None, :] <= cur[:, None]
        forced = (blk[None, :] == 0) | (blk[None, :] == cur[:, None]) | (blk[None, :] == cur[:, None] - 1)
        score = jnp.where(forced, FORCE, jnp.where(valid_s, imp_slc, NEG))
        _, sel = lax.top_k(score, n_sel)
        k_sel = _gather_blocks(k_slc, sel).reshape(B, G, Q_BLOCK, n_sel * SLC_BLOCK, DH)
        v_sel = _gather_blocks(v_slc, sel).reshape(B, G, Q_BLOCK, n_sel * SLC_BLOCK, DH)
        pos = (sel[..., None] * SLC_BLOCK + jnp.arange(SLC_BLOCK)).reshape(B, G, Q_BLOCK, n_sel * SLC_BLOCK)
        rel_s = t[None, None, :, None] - pos
        bias_s = jnp.moveaxis(tb[t5_bucket(rel_s), g_ix], -1, 2)
        s_s = jnp.einsum('btghd,bgtkd->bghtk', qb, k_sel).astype(jnp.float32) * scale
        s_s = jnp.where((rel_s >= 0)[:, :, None], s_s + bias_s, NEG)
        p_s = jax.nn.softmax(s_s, axis=-1)
        o_s = jnp.einsum('bghtk,bgtkd->btghd', p_s.astype(v_sel.dtype), v_sel)
        k_w = lax.dynamic_slice_in_dim(k_win, q0, Q_BLOCK + WINDOW, axis=1)
        v_w = lax.dynamic_slice_in_dim(v_win, q0, Q_BLOCK + WINDOW, axis=1)
        valid_w = (win_rel >= 0) & (win_rel < WINDOW) & ((q0 - WINDOW + k_loc) >= 0)[None, :]
        s_w = jnp.einsum('btghd,bkgd->bghtk', qb, k_w).astype(jnp.float32) * scale
        s_w = jnp.where(valid_w, s_w + win_bias, NEG)
        p_w = jax.nn.softmax(s_w, axis=-1)
        o_w = jnp.einsum('bghtk,bkgd->btghd', p_w.astype(v_w.dtype), v_w)
        o = gb[..., 0:1] * o_c + gb[..., 1:2] * o_s + gb[..., 2:3] * o_w
        return o.astype(h.dtype)

    o = lax.map(block_fn, (q_blocks, g_blocks, jnp.arange(nq)))
    o = jnp.moveaxis(o, 0, 1).reshape(B, S, N_HEADS * DH)
    return o @ w_out


def setup_inputs(seed: int = 0) -> dict:
    key = jax.random.key(seed)
    ks = iter(jax.random.split(key, 40))

    def nrm(shape, scale):
        return jax.random.normal(next(ks), shape, jnp.float32) * scale

    D = D_MODEL
    QW = N_HEADS * HEAD_DIM
    KVW = 2 * N_BRANCH * N_KV_GROUPS * HEAD_DIM
    return {
        "x": nrm((BATCH, SEQ, D), 1.0),
        "c": nrm((BATCH, D), 1.0),
        "mod_w": nrm((DEPTH, 2, D, 3 * D), D ** -0.5),
        "mod_b": nrm((DEPTH, 2, 3 * D), 0.01),
        "norm_pre": 1.0 + nrm((DEPTH, 2, D), 0.1),
        "norm_post": 1.0 + nrm((DEPTH, 2, D), 0.1),
        "a_w_in": nrm((N_A_LAYERS, D, 2 * A_WIDTH), D ** -0.5),
        "a_b_in": nrm((N_A_LAYERS, 2 * A_WIDTH), 0.01),
        "a_ln_g": 1.0 + nrm((N_A_LAYERS, A_WIDTH), 0.1),
        "a_ln_b": nrm((N_A_LAYERS, A_WIDTH), 0.01),
        "a_w_s": nrm((N_A_LAYERS, A_GROUPS, A_CHUNK, A_CHUNK), A_CHUNK ** -0.5),
        "a_b_s": 1.0 + nrm((N_A_LAYERS, A_GROUPS, A_CHUNK), 0.1),
        "a_w_out": nrm((N_A_LAYERS, A_WIDTH, D), A_WIDTH ** -0.5),
        "f_w_gate": nrm((DEPTH, D, D_FF), D ** -0.5),
        "f_w_up": nrm((DEPTH, D, D_FF), D ** -0.5),
        "f_conv_w": nrm((DEPTH, CONV_WIDTH, D_FF), CONV_WIDTH ** -0.5),
        "f_conv_b": nrm((DEPTH, D_FF), 0.01),
        "f_w_down": nrm((DEPTH, D_FF, D), D_FF ** -0.5),
        "kv_norm": 1.0 + nrm((D,), 0.1),
        "kv_mod_w": nrm((D, 2 * D), D ** -0.5),
        "kv_mod_b": nrm((2 * D,), 0.01),
        "w_kv": nrm((D, KVW), D ** -0.5),
        "cmp_pe": nrm((2, CMP_LEN, HEAD_DIM), 0.1),
        "cmp_w1": nrm((2, CMP_LEN * HEAD_DIM, HEAD_DIM), (CMP_LEN * HEAD_DIM) ** -0.5),
        "cmp_b1": nrm((2, HEAD_DIM), 0.01),
        "cmp_w2": nrm((2, HEAD_DIM, HEAD_DIM), HEAD_DIM ** -0.5),
        "b_w_in": nrm((N_B_LAYERS, D, QW + N_BRANCH * N_HEADS), D ** -0.5),
        "b_b_in": nrm((N_B_LAYERS, QW + N_BRANCH * N_HEADS), 0.01),
        "b_w_out": nrm((N_B_LAYERS, QW, D), QW ** -0.5),
        "rel_bias": nrm((N_BUCKETS, N_HEADS), 0.5),
    }


def reference(x, c, mod_w, mod_b, norm_pre, norm_post, a_w_in, a_b_in, a_ln_g, a_ln_b, a_w_s, a_b_s,
              a_w_out, f_w_gate, f_w_up, f_conv_w, f_conv_b, f_w_down, kv_norm, kv_mod_w, kv_mod_b,
              w_kv, cmp_pe, cmp_w1, cmp_b1, cmp_w2, b_w_in, b_b_in, b_w_out, rel_bias):
    c_act = jax.nn.silu(c)
    kvs = None
    for layer in range(DEPTH):
        if layer < N_A_LAYERS:
            i = layer
            mixer = lambda h, i=i: chunked_gmlp(h, a_w_in[i], a_b_in[i], a_ln_g[i], a_ln_b[i],
                                                a_w_s[i], a_b_s[i], a_w_out[i])
        else:
            if kvs is None:
                kvs = shared_kv(x, c_act, kv_norm, kv_mod_w, kv_mod_b, w_kv,
                                cmp_pe, cmp_w1, cmp_b1, cmp_w2)
            j = layer - N_A_LAYERS
            mixer = lambda h, j=j, kvs=kvs: nsa_attention(h, kvs, b_w_in[j], b_b_in[j], b_w_out[j], rel_bias)
        x = residual_sublayer(x, c_act, mod_w[layer, 0], mod_b[layer, 0],
                              norm_pre[layer, 0], norm_post[layer, 0], mixer)
        ffn = lambda h, l=layer: conv_ffn(h, f_w_gate[l], f_w_up[l], f_conv_w[l], f_conv_b[l], f_w_down[l])
        x = residual_sublayer(x, c_act, mod_w[layer, 1], mod_b[layer, 1],
                              norm_pre[layer, 1], norm_post[layer, 1], ffn)
    return x
```

```python
import functools
import math

import jax
import jax.numpy as jnp
from jax import lax
from jax.experimental import pallas as pl
from jax.experimental.pallas import tpu as pltpu

EPS = 1e-6
NEG = -1e30
FORCE = 1e30
M_INIT = -1e29

HEAD_DIM = 128
N_KV_GROUPS = 4
HEADS_PER_GROUP = 8
N_BRANCH = 3
CMP_LEN = 32
CMP_STRIDE = 16
SLC_BLOCK = 64
SLC_TOPN = 16
WINDOW = 512
Q_BLOCK = 128
A_CHUNK = 128
N_BUCKETS = 32
MAX_DISTANCE = 1024
CONV_WIDTH = 3

LANE = 128
KEY_TILE = 128
N_SEL_TABLES = 9
WIN_EDGE_TABLE = 9
F_ALIGN = 1024

_MIB = 1024 * 1024


def _cparams(sem, vmem_mib):
    return pltpu.CompilerParams(dimension_semantics=sem, vmem_limit_bytes=vmem_mib * _MIB)


def _gelu_tanh(x):
    return 0.5 * x * (1.0 + jnp.tanh(math.sqrt(2.0 / math.pi) * (x + 0.044715 * (x * x * x))))


def _sigmoid(x):
    return 1.0 / (1.0 + jnp.exp(-x))


def _pick(n, pref):
    t = min(pref, n)
    while n % t:
        t //= 2
    return t


def _mm_body(a_ref, b_ref, *rest, nk, has_bias, act, out_scale, split):
    if has_bias:
        bias_ref, o_ref, acc_ref = rest
    else:
        o_ref, acc_ref = rest
    k = pl.program_id(2)
    prod = jnp.dot(a_ref[...], b_ref[...], preferred_element_type=jnp.float32)

    @pl.when(k == 0)
    def _():
        acc_ref[...] = prod

    @pl.when(k > 0)
    def _():
        acc_ref[...] += prod

    @pl.when(k == nk - 1)
    def _():
        r = acc_ref[...]
        if has_bias:
            r = r + bias_ref[...]
        if out_scale is not None:
            r = r * out_scale
        if act == "gelu":
            r = _gelu_tanh(r)
        if split:
            for j in range(o_ref.shape[0]):
                o_ref[j] = r[:, j * LANE:(j + 1) * LANE].astype(o_ref.dtype)
        else:
            o_ref[...] = r.astype(o_ref.dtype)


def _matmul(a, b, *, name, out_dtype, bias=None, act=None, out_scale=None, split=False,
            tm=1024, tn=1024, tk=1024):
    m, kd = a.shape
    n = b.shape[1]
    tm, tn, tk = _pick(m, tm), _pick(n, tn), _pick(kd, tk)
    nk = kd // tk
    in_specs = [pl.BlockSpec((tm, tk), lambda i, j, k: (i, k)),
                pl.BlockSpec((tk, tn), lambda i, j, k: (k, j))]
    args = [a, b]
    if bias is not None:
        in_specs.append(pl.BlockSpec((1, tn), lambda i, j, k: (0, j)))
        args.append(bias.reshape(1, n).astype(jnp.float32))
    if split:
        out_shape = jax.ShapeDtypeStruct((n // LANE, m, LANE), out_dtype)
        out_spec = pl.BlockSpec((tn // LANE, tm, LANE), lambda i, j, k: (j, i, 0))
    else:
        out_shape = jax.ShapeDtypeStruct((m, n), out_dtype)
        out_spec = pl.BlockSpec((tm, tn), lambda i, j, k: (i, j))
    body = functools.partial(_mm_body, nk=nk, has_bias=bias is not None, act=act,
                             out_scale=out_scale, split=split)
    return pl.pallas_call(
        body, grid=(m // tm, n // tn, nk), in_specs=in_specs, out_specs=out_spec,
        out_shape=out_shape, scratch_shapes=[pltpu.VMEM((tm, tn), jnp.float32)],
        compiler_params=_cparams(("parallel", "parallel", "arbitrary"), 48), name=name,
    )(*args)


def _gemv_body(cb_ref, w_ref, b_ref, o_ref):
    c = cb_ref[...]
    ca = c * _sigmoid(c)
    for j in range(o_ref.shape[2] // LANE):
        sl = slice(j * LANE, (j + 1) * LANE)
        o_ref[0, :, sl] = jnp.sum(w_ref[0, :, sl] * ca, axis=0, keepdims=True) + b_ref[0, :, sl]


def _gemv(cb, w, b, *, name):
    ns, kd, n = w.shape
    tn = _pick(n, 512)
    return pl.pallas_call(
        _gemv_body, grid=(ns, n // tn),
        in_specs=[pl.BlockSpec((kd, LANE), lambda i, j: (0, 0)),
                  pl.BlockSpec((1, kd, tn), lambda i, j: (i, 0, j)),
                  pl.BlockSpec((1, 1, tn), lambda i, j: (i, 0, j))],
        out_specs=pl.BlockSpec((1, 1, tn), lambda i, j: (i, 0, j)),
        out_shape=jax.ShapeDtypeStruct((ns, 1, n), jnp.float32),
        compiler_params=_cparams(("parallel", "parallel"), 40), name=name,
    )(cb, w, b)


def _rms(v, g):
    return v * lax.rsqrt(jnp.mean(v * v, axis=-1, keepdims=True) + EPS) * g


def _row_body(*refs, has_res, n_h):
    x = refs[0][...]
    pos = 1
    if has_res:
        y_ref, gate_ref, gpost_ref = refs[1:4]
        pos = 4
    pre = refs[pos:pos + 3 * n_h]
    outs = refs[pos + 3 * n_h:]
    if has_res:
        x = x + gate_ref[...] * _rms(y_ref[...].astype(jnp.float32), gpost_ref[...])
        outs[0][...] = x
        outs = outs[1:]
    for i in range(n_h):
        g_ref, sc_ref, sh_ref = pre[3 * i:3 * i + 3]
        outs[i][...] = (_rms(x, g_ref[...]) * (1.0 + sc_ref[...]) + sh_ref[...]).astype(outs[i].dtype)


def _rows(x, *, name, res=None, pre=()):
    s, d = x.shape
    tr = _pick(s, 256)
    row = pl.BlockSpec((tr, d), lambda i: (i, 0))
    vec = pl.BlockSpec((1, d), lambda i: (0, 0))
    args, in_specs, out_shape, out_specs = [x], [row], [], []
    if res is not None:
        y, gate, gpost = res
        args += [y, gate.reshape(1, d), gpost.reshape(1, d)]
        in_specs += [row, vec, vec]
        out_shape.append(jax.ShapeDtypeStruct((s, d), jnp.float32))
        out_specs.append(row)
    for g, sc, sh in pre:
        args += [g.reshape(1, d), sc.reshape(1, d), sh.reshape(1, d)]
        in_specs += [vec, vec, vec]
        out_shape.append(jax.ShapeDtypeStruct((s, d), jnp.bfloat16))
        out_specs.append(row)
    outs = pl.pallas_call(
        functools.partial(_row_body, has_res=res is not None, n_h=len(pre)),
        grid=(s // tr,), in_specs=in_specs, out_specs=out_specs, out_shape=out_shape,
        compiler_params=_cparams(("parallel",), 48), name=name,
    )(*args)
    return outs


def _gmlp_mid_body(z_ref, lng_ref, lnb_ref, ws_ref, bs_ref, o_ref, vb_s, *, aw):
    v = z_ref[:, aw:].astype(jnp.float32)
    mu = jnp.mean(v, axis=-1, keepdims=True)
    vc = v - mu
    var = jnp.mean(vc * vc, axis=-1, keepdims=True)
    vb_s[...] = (vc * lax.rsqrt(var + EPS) * lng_ref[...] + lnb_ref[...]).astype(vb_s.dtype)
    tri = (lax.broadcasted_iota(jnp.int32, (A_CHUNK, A_CHUNK), 0)
           >= lax.broadcasted_iota(jnp.int32, (A_CHUNK, A_CHUNK), 1))
    for g in range(aw // LANE):
        sl = slice(g * LANE, (g + 1) * LANE)
        w = jnp.where(tri, ws_ref[g], jnp.zeros_like(ws_ref[g]))
        mixed = jnp.dot(w, vb_s[:, sl], preferred_element_type=jnp.float32) + bs_ref[g]
        o_ref[:, sl] = (z_ref[:, sl].astype(jnp.float32) * mixed).astype(o_ref.dtype)


def _gmlp_mid(z, ln_g, ln_b, ws, bs_b):
    s, aw2 = z.shape
    aw = aw2 // 2
    ng = aw // LANE
    return pl.pallas_call(
        functools.partial(_gmlp_mid_body, aw=aw), grid=(s // A_CHUNK,),
        in_specs=[pl.BlockSpec((A_CHUNK, aw2), lambda i: (i, 0)),
                  pl.BlockSpec((1, aw), lambda i: (0, 0)),
                  pl.BlockSpec((1, aw), lambda i: (0, 0)),
                  pl.BlockSpec((ng, A_CHUNK, A_CHUNK), lambda i: (0, 0, 0)),
                  pl.BlockSpec((ng, A_CHUNK, LANE), lambda i: (0, 0, 0))],
        out_specs=pl.BlockSpec((A_CHUNK, aw), lambda i: (i, 0)),
        out_shape=jax.ShapeDtypeStruct((s, aw), jnp.bfloat16),
        scratch_shapes=[pltpu.VMEM((A_CHUNK, aw), jnp.bfloat16)],
        compiler_params=_cparams(("parallel",), 32), name="gmlp_mid",
    )(z, ln_g.reshape(1, aw), ln_b.reshape(1, aw), ws, bs_b)


HALO = 16


def _conv_body(a_ref, ap_ref, u_ref, cw_ref, cb_ref, o_ref):
    a = a_ref[...].astype(jnp.float32)
    prev = jnp.where(pl.program_id(0) > 0, ap_ref[...].astype(jnp.float32), 0.0)
    row = lax.broadcasted_iota(jnp.int32, (HALO, a.shape[1]), 0)
    a1 = pltpu.roll(a, 1, 0)
    a2 = pltpu.roll(a, 2, 0)
    h1 = jnp.where(row < 1, pltpu.roll(prev, 1, 0), a1[:HALO])
    h2 = jnp.where(row < 2, pltpu.roll(prev, 2, 0), a2[:HALO])
    a1 = jnp.concatenate([h1, a1[HALO:]], axis=0)
    a2 = jnp.concatenate([h2, a2[HALO:]], axis=0)
    conv = a2 * cw_ref[0:1, :] + a1 * cw_ref[1:2, :] + a * cw_ref[2:3, :] + cb_ref[...]
    o_ref[...] = (conv * _sigmoid(conv) * u_ref[...].astype(jnp.float32)).astype(o_ref.dtype)


def _conv_gate(a, u, cw, cb):
    s, f = a.shape
    tm, tn = _pick(s, 512), _pick(f, 1024)
    hb = tm // HALO
    return pl.pallas_call(
        _conv_body, grid=(s // tm, f // tn),
        in_specs=[pl.BlockSpec((tm, tn), lambda i, j: (i, j)),
                  pl.BlockSpec((HALO, tn), lambda i, j: (jnp.maximum(i * hb - 1, 0), j)),
                  pl.BlockSpec((tm, tn), lambda i, j: (i, j)),
                  pl.BlockSpec((CONV_WIDTH, tn), lambda i, j: (0, j)),
                  pl.BlockSpec((1, tn), lambda i, j: (0, j))],
        out_specs=pl.BlockSpec((tm, tn), lambda i, j: (i, j)),
        out_shape=jax.ShapeDtypeStruct((s, f), jnp.bfloat16),
        compiler_params=_cparams(("parallel", "parallel"), 32), name="ffn_conv_gate",
    )(a, a, u, cw, cb.reshape(1, f))


def _compress_body(r_ref, w1_ref, pe_ref, b1_ref, w2_ref, o_ref):
    r = r_ref[0]
    half = r.shape[1]
    p1 = jnp.dot(r, w1_ref[0, :half, :], preferred_element_type=jnp.float32)
    p2 = jnp.dot(r, w1_ref[0, half:, :], preferred_element_type=jnp.float32)
    n_w = r.shape[0]
    p2 = pltpu.roll(p2, n_w - 1, 0)
    pe = jnp.broadcast_to(pe_ref[0], (8, pe_ref.shape[2])).astype(jnp.bfloat16)
    cst = jnp.dot(pe, w1_ref[0], preferred_element_type=jnp.float32)[0:1] + b1_ref[0]
    hid = _gelu_tanh(p1 + p2 + cst)
    out = jnp.dot(hid.astype(jnp.bfloat16), w2_ref[0], preferred_element_type=jnp.float32)
    row = lax.broadcasted_iota(jnp.int32, out.shape, 0)
    o_ref[0] = jnp.where(row < n_w - 1, out, 0.0).astype(o_ref.dtype)


def _compress(kvt_r, w1, pe, b1, w2):
    n_w, wd = kvt_r.shape[1:]
    g = N_KV_GROUPS
    return pl.pallas_call(
        _compress_body, grid=(2 * g,),
        in_specs=[pl.BlockSpec((1, n_w, wd), lambda i: (i, 0, 0)),
                  pl.BlockSpec((1, 2 * wd, HEAD_DIM), lambda i: (i // g, 0, 0)),
                  pl.BlockSpec((1, 1, 2 * wd), lambda i: (i // g, 0, 0)),
                  pl.BlockSpec((1, 1, HEAD_DIM), lambda i: (i // g, 0, 0)),
                  pl.BlockSpec((1, HEAD_DIM, HEAD_DIM), lambda i: (i // g, 0, 0))],
        out_specs=pl.BlockSpec((1, n_w, HEAD_DIM), lambda i: (i, 0, 0)),
        out_shape=jax.ShapeDtypeStruct((2 * g, n_w, HEAD_DIM), jnp.bfloat16),
        compiler_params=_cparams(("parallel",), 32), name="kv_compress",
    )(kvt_r, w1, pe, b1, w2)


def _nsa_body(q_ref, gl_ref, ks_ref, vs_ref, kw_ref, vw_ref, kc_ref, vc_ref, stab_ref, ctab_ref,
              o_ref, q_s, imp_s, st_s, sel_s, m_s, l_s, acc_s, oc_s, os_s, *, nct):
    hg = HEADS_PER_GROUP
    bi = pl.program_id(1)
    for h in range(hg):
        q_s[h] = q_ref[:, h * HEAD_DIM:(h + 1) * HEAD_DIM]
    nt_dims = (((1,), (1,)), ((), ()))

    c_lo = ((bi + 8) >> 4) - 1
    imp_s[...] = jnp.zeros_like(imp_s)

    def cmp_body(h, carry):
        sc = lax.dot_general(q_s[h], kc_ref[0], nt_dims, preferred_element_type=jnp.float32)
        hrow = pl.ds(pl.multiple_of(h * Q_BLOCK, Q_BLOCK), Q_BLOCK)
        far = stab_ref[0, N_SEL_TABLES - 1, hrow, :]
        tab = ctab_ref[0, 0, h]
        tiles = []
        for c in range(nct):
            t = jnp.where(c < c_lo, far,
                          jnp.where(c == c_lo, tab[:, :LANE],
                                    jnp.where(c == c_lo + 1, tab[:, LANE:], NEG)))
            tiles.append(sc[:, c * LANE:(c + 1) * LANE] + t)
        s = jnp.concatenate(tiles, axis=1) if nct > 1 else tiles[0]
        m = jnp.maximum(jnp.max(s, axis=1, keepdims=True), M_INIT)
        e = jnp.exp(s - m)
        l = jnp.sum(e, axis=1, keepdims=True)
        p = e / jnp.where(l > 0.0, l, 1.0)
        imp_s[...] += p
        oc_s[hrow, :] = jnp.dot(p.astype(jnp.bfloat16), vc_ref[0], preferred_element_type=jnp.float32)
        return carry

    lax.fori_loop(0, hg, cmp_body, 0)

    ncmp = nct * LANE
    jn = lax.broadcasted_iota(jnp.int32, (LANE, ncmp), 0)
    nn = lax.broadcasted_iota(jnp.int32, (LANE, ncmp), 1)
    m_sub = SLC_BLOCK // CMP_STRIDE
    m_len = CMP_LEN // CMP_STRIDE
    pool = ((nn >= m_sub * jn - (m_len - 1)) & (nn <= m_sub * jn + m_sub - 1)).astype(jnp.float32)
    imp_t = lax.dot_general(pool, imp_s[...], nt_dims, preferred_element_type=jnp.float32,
                            precision=lax.Precision.HIGHEST)
    jio = lax.broadcasted_iota(jnp.int32, (LANE, Q_BLOCK), 0)
    tio = lax.broadcasted_iota(jnp.int32, (LANE, Q_BLOCK), 1)
    per_q = Q_BLOCK // SLC_BLOCK
    cur = per_q * bi + tio // SLC_BLOCK
    forced = (jio == 0) | (jio == cur) | (jio == cur - 1)
    valid = jio <= cur
    score = jnp.where(forced, FORCE, jnp.where(valid, imp_t, NEG))
    st_s[...] = score

    def rank_body(i, cnt):
        row = st_s[pl.ds(i, 1), :]
        beats = (row > score) | ((row == score) & (jio > i))
        return cnt + jnp.where(beats, 1.0, 0.0)

    cnt = lax.fori_loop(0, per_q * (bi + 1), rank_body, jnp.zeros((LANE, Q_BLOCK), jnp.float32))
    sel_t = jnp.where((cnt < float(SLC_TOPN)) & valid, 1.0, 0.0)
    sel_s[...] = sel_t.T.astype(sel_s.dtype)

    def flash(k_ref, v_ref, lo, hi, use_sel):
        m_s[...] = jnp.full_like(m_s, M_INIT)
        l_s[...] = jnp.zeros_like(l_s)
        acc_s[...] = jnp.zeros_like(acc_s)

        def body(kj, carry):
            rows = pl.ds(pl.multiple_of(kj * KEY_TILE, KEY_TILE), KEY_TILE)
            qa = q_s[...].reshape(hg * Q_BLOCK, HEAD_DIM)
            s = lax.dot_general(qa, k_ref[0, rows, :], nt_dims, preferred_element_type=jnp.float32)
            d = bi - kj
            if use_sel:
                idx = jnp.minimum(d, N_SEL_TABLES - 1)
            else:
                idx = jnp.where(d == WINDOW // KEY_TILE, WIN_EDGE_TABLE, d)
            s = s + stab_ref[0, idx]
            if use_sel:
                per_k = KEY_TILE // SLC_BLOCK
                expand = jnp.where(jio == per_k * kj + tio // SLC_BLOCK, 1.0, 0.0).astype(jnp.bfloat16)
                hit = jnp.dot(sel_s[...], expand, preferred_element_type=jnp.float32)
                am = (hit - 1.0) * FORCE
                s = (s.reshape(hg, Q_BLOCK, KEY_TILE) + am[None]).reshape(hg * Q_BLOCK, KEY_TILE)
            m_prev = m_s[...]
            m_new = jnp.maximum(m_prev, jnp.max(s, axis=1, keepdims=True))
            alpha = jnp.exp(m_prev - m_new)
            p = jnp.exp(s - m_new)
            l_s[...] = alpha * l_s[...] + jnp.sum(p, axis=1, keepdims=True)
            acc_s[...] = alpha * acc_s[...] + jnp.dot(p.astype(jnp.bfloat16), v_ref[0, rows, :],
                                                      preferred_element_type=jnp.float32)
            m_s[...] = m_new
            return carry

        lax.fori_loop(lo, hi, body, 0)

    flash(ks_ref, vs_ref, 0, bi + 1, True)
    os_s[...] = acc_s[...] / l_s[...]
    flash(kw_ref, vw_ref, jnp.maximum(bi - WINDOW // KEY_TILE, 0), bi + 1, False)

    gs = _sigmoid(gl_ref[...])
    for h in range(hg):
        hrow = slice(h * Q_BLOCK, (h + 1) * Q_BLOCK)
        c0 = N_BRANCH * h
        o = (gs[:, c0:c0 + 1] * oc_s[hrow, :] + gs[:, c0 + 1:c0 + 2] * os_s[hrow, :]
             + gs[:, c0 + 2:c0 + 3] * (acc_s[hrow, :] / l_s[hrow, :]))
        o_ref[:, h * HEAD_DIM:(h + 1) * HEAD_DIM] = o.astype(o_ref.dtype)


def _nsa(q, gl, kvt, kcv, stab, ctab):
    s, qw = q.shape
    g, hg = N_KV_GROUPS, HEADS_PER_GROUP
    gw = hg * HEAD_DIM
    ncmp = kcv.shape[1]
    nct = ncmp // LANE
    nq = s // Q_BLOCK
    rows = hg * Q_BLOCK

    def kv_spec(base):
        return pl.BlockSpec((1, s, HEAD_DIM), lambda gi, bi: (base + gi, 0, 0))

    def c_spec(base):
        return pl.BlockSpec((1, ncmp, HEAD_DIM), lambda gi, bi: (base + gi, 0, 0))

    return pl.pallas_call(
        functools.partial(_nsa_body, nct=nct), grid=(g, nq),
        in_specs=[pl.BlockSpec((Q_BLOCK, gw), lambda gi, bi: (bi, gi)),
                  pl.BlockSpec((Q_BLOCK, LANE), lambda gi, bi: (bi, gi)),
                  kv_spec(2 * g), kv_spec(3 * g), kv_spec(4 * g), kv_spec(5 * g),
                  c_spec(0), c_spec(g),
                  pl.BlockSpec((1, N_SEL_TABLES + 1, rows, KEY_TILE), lambda gi, bi: (gi, 0, 0, 0)),
                  pl.BlockSpec((1, 1, hg, Q_BLOCK, 2 * LANE), lambda gi, bi: (gi, (bi + 8) % 16, 0, 0, 0))],
        out_specs=pl.BlockSpec((Q_BLOCK, gw), lambda gi, bi: (bi, gi)),
        out_shape=jax.ShapeDtypeStruct((s, qw), jnp.bfloat16),
        scratch_shapes=[pltpu.VMEM((hg, Q_BLOCK, HEAD_DIM), jnp.bfloat16),
                        pltpu.VMEM((Q_BLOCK, ncmp), jnp.float32),
                        pltpu.VMEM((LANE, Q_BLOCK), jnp.float32),
                        pltpu.VMEM((Q_BLOCK, LANE), jnp.bfloat16),
                        pltpu.VMEM((rows, 1), jnp.float32),
                        pltpu.VMEM((rows, 1), jnp.float32),
                        pltpu.VMEM((rows, HEAD_DIM), jnp.float32),
                        pltpu.VMEM((rows, HEAD_DIM), jnp.float32),
                        pltpu.VMEM((rows, HEAD_DIM), jnp.float32)],
        compiler_params=_cparams(("parallel", "arbitrary"), 56), name="nsa_attention",
    )(q, gl, kvt, kvt, kvt, kvt, kcv, kcv, stab, ctab)


def _t5_bucket(rel):
    n = jnp.maximum(rel, 0)
    max_exact = N_BUCKETS // 2
    nf = jnp.maximum(n, max_exact).astype(jnp.float32)
    large = max_exact + (jnp.log(nf / max_exact) / math.log(MAX_DISTANCE / max_exact)
                         * (N_BUCKETS - max_exact)).astype(jnp.int32)
    large = jnp.minimum(large, N_BUCKETS - 1)
    return jnp.where(n < max_exact, n, large)


def _bias_tables(rel_bias):
    g, hg = N_KV_GROUPS, HEADS_PER_GROUP
    tl = jnp.arange(Q_BLOCK)[:, None]

    def table(rel, valid):
        t = jnp.where(valid[..., None], rel_bias[_t5_bucket(rel)], NEG)
        return jnp.moveaxis(t, -1, 0).reshape((g, hg) + rel.shape)

    sl = jnp.arange(KEY_TILE)[None, :]
    tabs = []
    for d in range(N_SEL_TABLES):
        rel = KEY_TILE * d + tl - sl
        tabs.append(table(rel, rel >= 0))
    rel = WINDOW + tl - sl
    tabs.append(table(rel, (rel >= 0) & (rel < WINDOW)))
    stab = jnp.stack(tabs, axis=1).reshape(g, N_SEL_TABLES + 1, hg * Q_BLOCK, KEY_TILE)

    j = jnp.arange(2 * LANE)[None, :]
    ctabs = []
    for rho in range(16):
        rel = tl - CMP_STRIDE * j + Q_BLOCK * rho + (8 * Q_BLOCK - (CMP_LEN - 1))
        ctabs.append(table(rel, rel >= 0))
    ctab = jnp.stack(ctabs, axis=1)
    return stab, ctab


def _pad_cols(w, n):
    return jnp.pad(w, ((0, 0), (0, n - w.shape[1])))


def kernel(x, c, mod_w, mod_b, norm_pre, norm_post, a_w_in, a_b_in, a_ln_g, a_ln_b, a_w_s, a_b_s,
           a_w_out, f_w_gate, f_w_up, f_conv_w, f_conv_b, f_w_down, kv_norm, kv_mod_w, kv_mod_b,
           w_kv, cmp_pe, cmp_w1, cmp_b1, cmp_w2, b_w_in, b_b_in, b_w_out, rel_bias):
    bsz, s, d = x.shape
    assert bsz == 1
    bf = jnp.bfloat16
    g, hg = N_KV_GROUPS, HEADS_PER_GROUP
    qw = g * hg * HEAD_DIM
    f = f_w_gate.shape[-1]
    fp = -(-f // F_ALIGN) * F_ALIGN
    x2 = x.reshape(s, d)

    cb = jnp.broadcast_to(c.reshape(d, 1), (d, LANE))
    mods = _gemv(cb, mod_w.reshape(4, d, 3 * d), mod_b.reshape(4, 1, 3 * d), name="adaln_mod")
    kvmod = _gemv(cb, kv_mod_w.reshape(1, d, 2 * d), kv_mod_b.reshape(1, 1, 2 * d), name="adaln_kv")[0]

    def mod(i):
        m = mods[i]
        return m[:, :d], m[:, d:2 * d], m[:, 2 * d:]

    def ffn(h, layer):
        wg = _pad_cols(f_w_gate[layer], fp).astype(bf)
        wu = _pad_cols(f_w_up[layer], fp).astype(bf)
        wd = jnp.pad(f_w_down[layer], ((0, fp - f), (0, 0))).astype(bf)
        a = _matmul(h, wg, name="ffn_gate", out_dtype=bf)
        u = _matmul(h, wu, name="ffn_up", out_dtype=bf)
        gt = _conv_gate(a, u, _pad_cols(f_conv_w[layer], fp), _pad_cols(f_conv_b[layer][None], fp))
        return _matmul(gt, wd, name="ffn_down", out_dtype=jnp.float32)

    sh, sc, ga = mod(0)
    (h,) = _rows(x2, name="pre_mixer0", pre=[(norm_pre[0, 0], sc, sh)])
    z = _matmul(h, a_w_in[0].astype(bf), name="gmlp_in", out_dtype=bf, bias=a_b_in[0], act="gelu")
    ng = a_w_s.shape[1]
    bs_b = jnp.broadcast_to(a_b_s[0][:, :, None], (ng, A_CHUNK, LANE))
    um = _gmlp_mid(z, a_ln_g[0], a_ln_b[0], a_w_s[0].astype(bf), bs_b)
    y = _matmul(um, a_w_out[0].astype(bf), name="gmlp_out", out_dtype=jnp.float32)
    sh2, sc2, ga2 = mod(1)
    x2, h = _rows(x2, name="post_mixer0", res=(y, ga, norm_post[0, 0]), pre=[(norm_pre[0, 1], sc2, sh2)])
    y = ffn(h, 0)

    sh3, sc3, ga3 = mod(2)
    kv_sh, kv_sc = kvmod[:, :d], kvmod[:, d:]
    x2, h, hkv = _rows(x2, name="post_ffn0", res=(y, ga2, norm_post[0, 1]),
                       pre=[(norm_pre[1, 0], sc3, sh3), (kv_norm, kv_sc, kv_sh)])
    kvt = _matmul(hkv, w_kv.astype(bf), name="kv_proj", out_dtype=bf, split=True)
    n_w = s // CMP_STRIDE
    kcv = _compress(kvt.reshape(kvt.shape[0], n_w, CMP_STRIDE * HEAD_DIM), cmp_w1.astype(bf),
                    cmp_pe.reshape(2, 1, CMP_LEN * HEAD_DIM), cmp_b1.reshape(2, 1, HEAD_DIM),
                    cmp_w2.astype(bf))

    w_in, b_in = b_w_in[0], b_b_in[0]
    q = _matmul(h, w_in[:, :qw].astype(bf), name="nsa_q", out_dtype=bf, bias=b_in[:qw],
                out_scale=HEAD_DIM ** -0.5)
    ngc = hg * N_BRANCH
    wgt = jnp.pad(w_in[:, qw:].reshape(d, g, ngc), ((0, 0), (0, 0), (0, LANE - ngc))).reshape(d, g * LANE)
    bgt = jnp.pad(b_in[qw:].reshape(g, ngc), ((0, 0), (0, LANE - ngc))).reshape(g * LANE)
    gl = _matmul(h, wgt.astype(bf), name="nsa_gate", out_dtype=jnp.float32, bias=bgt)
    stab, ctab = _bias_tables(rel_bias)
    o = _nsa(q, gl, kvt, kcv, stab, ctab)
    y = _matmul(o, b_w_out[0].astype(bf), name="nsa_out", out_dtype=jnp.float32)
    sh4, sc4, ga4 = mod(3)
    x2, h = _rows(x2, name="post_mixer1", res=(y, ga3, norm_post[1, 0]), pre=[(norm_pre[1, 1], sc4, sh4)])
    y = ffn(h, 1)
    (x2,) = _rows(x2, name="post_ffn1", res=(y, ga4, norm_post[1, 1]))
    return x2.reshape(bsz, s, d)
```
